```python
import jax, jax.numpy as jnp
from jax import lax
import numpy as np

D_MODEL = 2048
BATCH = 4
SEQ = 2048
DEPTH = 4
DEC_BATCH = 8
DEC_SEQ = 1
PAST_LEN = 16384
PAGE_SIZE = 128

HEAD_DIM = 128
HEADS_PER_GROUP = D_MODEL // 256
GROUPS = ((128, 1), (512, 4), (2048, 16))
N_GROUPS = 3
ATTN_WIDTH = N_GROUPS * HEADS_PER_GROUP * HEAD_DIM
D_CONV = D_MODEL // 2
CONV_WIDTH = 31
D_FF = ((8 * D_MODEL // 3 + 255) // 256) * 256
N_EXPERTS = 8
TOP_K = 2
D_FF_EXPERT = 7 * D_MODEL // 2
MOE_BLOCK = 256
PLE_DIM = 256
ROPE_THETA = 10000.0
EPS = 1e-6
D_IN = 3 * ATTN_WIDTH + 2 * D_CONV + 2 * D_MODEL
IN_SPLITS = (ATTN_WIDTH, 2 * ATTN_WIDTH, 3 * ATTN_WIDTH, 3 * ATTN_WIDTH + 2 * D_CONV, 3 * ATTN_WIDTH + 2 * D_CONV + D_MODEL)
N_DENSE = (DEPTH + 1) // 2
N_MOE = DEPTH // 2

kernel_name = 'dilated_attn_conformer_hybrid_step'


def window_buffer_len(window):
    return min(window, PAST_LEN)


def rmsnorm(x, g):
    x32 = x.astype(jnp.float32)
    y = x32 * lax.rsqrt(jnp.mean(x32 * x32, axis=-1, keepdims=True) + EPS) * g.astype(jnp.float32)
    return y.astype(x.dtype)


def layernorm(x, g, b):
    x32 = x.astype(jnp.float32)
    mu = jnp.mean(x32, axis=-1, keepdims=True)
    var = jnp.mean(jnp.square(x32 - mu), axis=-1, keepdims=True)
    y = (x32 - mu) * lax.rsqrt(var + EPS) * g.astype(jnp.float32) + b.astype(jnp.float32)
    return y.astype(x.dtype)


def rope(t, pos):
    half = t.shape[-1] // 2
    inv = ROPE_THETA ** (-jnp.arange(half, dtype=jnp.float32) / half)
    ang = pos.astype(jnp.float32)[:, None] * inv[None, :]
    cos, sin = jnp.cos(ang)[:, None, :], jnp.sin(ang)[:, None, :]
    t32 = t.astype(jnp.float32)
    t1, t2 = t32[..., :half], t32[..., half:]
    return jnp.concatenate([t1 * cos - t2 * sin, t2 * cos + t1 * sin], axis=-1).astype(t.dtype)


def dilated_attention_prompt(q, k, v, window, dil):
    N, S, H, D = q.shape
    M = window // dil
    L = S // dil
    Lp = -(-L // M) * M
    nb = Lp // M

    def residues(t):
        t = t.reshape(N, L, dil, H, D).transpose(0, 2, 1, 3, 4)
        return jnp.pad(t, ((0, 0), (0, 0), (0, Lp - L), (0, 0), (0, 0)))

    def band(t):
        t = jnp.pad(residues(t), ((0, 0), (0, 0), (M, 0), (0, 0), (0, 0)))
        return jnp.concatenate([t[:, :, :Lp].reshape(N, dil, nb, M, H, D),
                                t[:, :, M:].reshape(N, dil, nb, M, H, D)], axis=3)

    qb = residues(q).reshape(N, dil, nb, M, H, D).astype(jnp.float32)
    kw, vw = band(k).astype(jnp.float32), band(v).astype(jnp.float32)
    s = jnp.einsum('brnqhd,brnkhd->brnhqk', qb, kw) * (D ** -0.5)
    qi = jnp.arange(M)[:, None]
    kj = jnp.arange(2 * M)[None, :]
    dist = qi + M - kj
    blk = jnp.arange(nb)[:, None, None]
    mask = ((dist >= 0) & (dist <= M))[None] & (blk * M + kj[None] - M >= 0)
    s = jnp.where(mask[None, None, :, None], s, -jnp.inf)
    lse = jax.nn.logsumexp(s, axis=-1)
    pr = jnp.exp(s - lse[..., None])
    o = jnp.einsum('brnhqk,brnkhd->brnqhd', pr, vw)
    o = o.reshape(N, dil, Lp, H, D)[:, :, :L].transpose(0, 2, 1, 3, 4).reshape(N, S, H, D)
    lse = lse.transpose(0, 1, 2, 4, 3).reshape(N, dil, Lp, H)[:, :, :L].transpose(0, 2, 1, 3).reshape(N, S, H)
    return o, lse


def dilated_attention_step(q, k, v, k_buf, v_buf, window, dil):
    N, S, H, D = q.shape
    WB = k_buf.shape[1]
    M = window // dil
    kc = jnp.concatenate([k_buf, k], axis=1)
    vc = jnp.concatenate([v_buf, v], axis=1)
    idx = WB + jnp.arange(S)[:, None] - dil * jnp.arange(M + 1)[None, :]
    valid = idx >= 0
    idx = jnp.maximum(idx, 0)
    kg = kc[:, idx].astype(jnp.float32)
    vg = vc[:, idx].astype(jnp.float32)
    s = jnp.einsum('nshd,nsmhd->nshm', q.astype(jnp.float32), kg) * (D ** -0.5)
    s = jnp.where(valid[None, :, None, :], s, -jnp.inf)
    lse = jax.nn.logsumexp(s, axis=-1)
    pr = jnp.exp(s - lse[..., None])
    o = jnp.einsum('nshm,nsmhd->nshd', pr, vg)
    return o, lse, kc[:, S:], vc[:, S:]


def last_rows(t, n):
    return jnp.pad(t, ((0, 0), (n, 0), (0, 0), (0, 0)))[:, -n:]


def causal_depthwise_conv(u_ext, w, b):
    C = u_ext.shape[-1]
    y = lax.conv_general_dilated(u_ext, w[:, None, :].astype(u_ext.dtype), window_strides=(1,),
                                 padding='VALID', dimension_numbers=('NWC', 'WIO', 'NWC'),
                                 feature_group_count=C)
    return y + b


def swiglu(x, wg, wu, wd):
    return (jax.nn.silu(x @ wg) * (x @ wu)) @ wd


def moe_swiglu(h, w_router, wg, wu, wd):
    N, S, D = h.shape
    T = N * S
    A = T * TOP_K
    xt = h.reshape(T, D)
    logits = (xt @ w_router).astype(jnp.float32)
    top_v, top_i = lax.top_k(logits, TOP_K)
    gates = jax.nn.softmax(top_v, axis=-1)
    e_flat = top_i.reshape(-1).astype(jnp.int32)
    tok_flat = jnp.repeat(jnp.arange(T, dtype=jnp.int32), TOP_K)
    g_flat = gates.reshape(-1)
    bk = min(MOE_BLOCK, A)
    nblk = -(-A // bk) + N_EXPERTS
    order = jnp.argsort(e_flat)
    e_s, tok_s, g_s = e_flat[order], tok_flat[order], g_flat[order]
    counts = jnp.bincount(e_flat, length=N_EXPERTS).astype(jnp.int32)
    padded = ((counts + bk - 1) // bk) * bk
    pad_end = jnp.cumsum(padded)
    pad_start = pad_end - padded
    start = jnp.cumsum(counts) - counts
    dest = pad_start[e_s] + jnp.arange(A, dtype=jnp.int32) - start[e_s]
    xbuf = jnp.zeros((nblk * bk, D), h.dtype).at[dest].set(xt[tok_s])
    block_e = jnp.minimum(jnp.searchsorted(pad_end, jnp.arange(nblk, dtype=jnp.int32) * bk, side='right'),
                          N_EXPERTS - 1)

    def expert_block(args):
        xb, e = args
        return swiglu(xb, wg[e], wu[e], wd[e])

    ybuf = lax.map(expert_block, (xbuf.reshape(nblk, bk, D), block_e)).reshape(nblk * bk, D)
    y = jax.ops.segment_sum(ybuf[dest] * g_s[:, None].astype(h.dtype), tok_s, num_segments=T)
    return y.reshape(N, S, D)


def run_trunk(x, p, pos, k_bufs, v_bufs, conv_buf, g_mix, w_in, w_ao, conv_w, conv_b, conv_ln_g, conv_ln_b,
              w_cp, w_o, g_ffn, w_ff_gate, w_ff_up, w_ff_down, w_router, w_e_gate, w_e_up, w_e_down,
              g_ple, w_pg, w_ple, g_final):
    is_prompt = k_bufs is None
    N, S, _ = x.shape
    new_k = [[] for _ in GROUPS]
    new_v = [[] for _ in GROUPS]
    new_conv = []
    for i in range(DEPTH):
        h = rmsnorm(x, g_mix[i])
        q, k, v, cu, ga, gc = jnp.split(h @ w_in[i], IN_SPLITS, axis=-1)
        q = rope(q.reshape(N, S, -1, HEAD_DIM), pos).reshape(N, S, N_GROUPS, HEADS_PER_GROUP, HEAD_DIM)
        k = rope(k.reshape(N, S, -1, HEAD_DIM), pos).reshape(N, S, N_GROUPS, HEADS_PER_GROUP, HEAD_DIM)
        v = v.reshape(N, S, N_GROUPS, HEADS_PER_GROUP, HEAD_DIM)
        outs, lses = [], []
        for g, (window, dil) in enumerate(GROUPS):
            qg, kg, vg = q[:, :, g], k[:, :, g], v[:, :, g]
            if is_prompt:
                o, l = dilated_attention_prompt(qg, kg, vg, window, dil)
                wb = window_buffer_len(window)
                kb, vb = last_rows(kg, wb), last_rows(vg, wb)
            else:
                o, l, kb, vb = dilated_attention_step(qg, kg, vg, k_bufs[g][i], v_bufs[g][i], window, dil)
            outs.append(o)
            lses.append(l)
            new_k[g].append(kb)
            new_v[g].append(vb)
        alpha = jax.nn.softmax(jnp.stack(lses, axis=0), axis=0)
        o = jnp.einsum('gnsh,gnshd->nshd', alpha, jnp.stack(outs, axis=0))
        a = o.reshape(N, S, HEADS_PER_GROUP * HEAD_DIM).astype(x.dtype) @ w_ao[i]
        u = cu[..., :D_CONV] * jax.nn.sigmoid(cu[..., D_CONV:])
        hist = jnp.zeros((N, CONV_WIDTH - 1, D_CONV), u.dtype) if is_prompt else conv_buf[i]
        u_ext = jnp.concatenate([hist, u], axis=1)
        new_conv.append(u_ext[:, -(CONV_WIDTH - 1):])
        c = layernorm(causal_depthwise_conv(u_ext, conv_w[i], conv_b[i]), conv_ln_g[i], conv_ln_b[i])
        c = jax.nn.silu(c) @ w_cp[i]
        x = x + (jax.nn.sigmoid(ga) * a + jax.nn.sigmoid(gc) * c) @ w_o[i]
        h2 = rmsnorm(x, g_ffn[i])
        j = i // 2
        if i % 2 == 0:
            x = x + swiglu(h2, w_ff_gate[j], w_ff_up[j], w_ff_down[j])
        else:
            x = x + moe_swiglu(h2, w_router[j], w_e_gate[j], w_e_up[j], w_e_down[j])
        x = x + jax.nn.sigmoid(rmsnorm(x, g_ple[i]) @ w_pg[i]) * (p[i] @ w_ple[i])
    y = rmsnorm(x, g_final)
    ks = [jnp.stack(l, axis=0) for l in new_k]
    vs = [jnp.stack(l, axis=0) for l in new_v]
    return y, ks, vs, jnp.stack(new_conv, axis=0)


def setup_inputs(seed: int = 0) -> dict:
    key = jax.random.key(seed)
    ks = iter(jax.random.split(key, 48))

    def nrm(shape, scale):
        return jax.random.normal(next(ks), shape, jnp.float32) * scale

    def gain(shape):
        return 1.0 + nrm(shape, 0.01)

    kv = HEADS_PER_GROUP * HEAD_DIM
    w1, w2, w3 = (window_buffer_len(w) for w, _ in GROUPS)
    return {
        'x_prompt': nrm((BATCH, SEQ, D_MODEL), 1.0),
        'x_sample': nrm((DEC_BATCH, DEC_SEQ, D_MODEL), 1.0),
        'cache_k_w128': nrm((DEPTH, DEC_BATCH, w1, HEADS_PER_GROUP, HEAD_DIM), 1.0),
        'cache_v_w128': nrm((DEPTH, DEC_BATCH, w1, HEADS_PER_GROUP, HEAD_DIM), 1.0),
        'cache_k_w512': nrm((DEPTH, DEC_BATCH, w2, HEADS_PER_GROUP, HEAD_DIM), 1.0),
        'cache_v_w512': nrm((DEPTH, DEC_BATCH, w2, HEADS_PER_GROUP, HEAD_DIM), 1.0),
        'cache_k_w2048': nrm((DEPTH, DEC_BATCH, w3, HEADS_PER_GROUP, HEAD_DIM), 1.0),
        'cache_v_w2048': nrm((DEPTH, DEC_BATCH, w3, HEADS_PER_GROUP, HEAD_DIM), 1.0),
        'state_conv': nrm((DEPTH, DEC_BATCH, CONV_WIDTH - 1, D_CONV), 0.5),
        'p_prompt': nrm((DEPTH, BATCH, SEQ, PLE_DIM), 1.0),
        'p_sample': nrm((DEPTH, DEC_BATCH, DEC_SEQ, PLE_DIM), 1.0),
        'g_mix': gain((DEPTH, D_MODEL)),
        'w_in': nrm((DEPTH, D_MODEL, D_IN), D_MODEL ** -0.5),
        'w_ao': nrm((DEPTH, kv, D_MODEL), kv ** -0.5),
        'conv_w': nrm((DEPTH, CONV_WIDTH, D_CONV), CONV_WIDTH ** -0.5),
        'conv_b': nrm((DEPTH, D_CONV), 0.01),
        'conv_ln_g': gain((DEPTH, D_CONV)),
        'conv_ln_b': nrm((DEPTH, D_CONV), 0.01),
        'w_cp': nrm((DEPTH, D_CONV, D_MODEL), D_CONV ** -0.5),
        'w_o': nrm((DEPTH, D_MODEL, D_MODEL), D_MODEL ** -0.5),
        'g_ffn': gain((DEPTH, D_MODEL)),
        'w_ff_gate': nrm((N_DENSE, D_MODEL, D_FF), D_MODEL ** -0.5),
        'w_ff_up': nrm((N_DENSE, D_MODEL, D_FF), D_MODEL ** -0.5),
        'w_ff_down': nrm((N_DENSE, D_FF, D_MODEL), D_FF ** -0.5),
        'w_router': nrm((N_MOE, D_MODEL, N_EXPERTS), D_MODEL ** -0.5),
        'w_e_gate': nrm((N_MOE, N_EXPERTS, D_MODEL, D_FF_EXPERT), D_MODEL ** -0.5),
        'w_e_up': nrm((N_MOE, N_EXPERTS, D_MODEL, D_FF_EXPERT), D_MODEL ** -0.5),
        'w_e_down': nrm((N_MOE, N_EXPERTS, D_FF_EXPERT, D_MODEL), D_FF_EXPERT ** -0.5),
        'g_ple': gain((DEPTH, D_MODEL)),
        'w_pg': nrm((DEPTH, D_MODEL, D_MODEL), D_MODEL ** -0.5),
        'w_ple': nrm((DEPTH, PLE_DIM, D_MODEL), PLE_DIM ** -0.5),
        'g_final': gain((D_MODEL,)),
    }


def reference(x_prompt, x_sample, cache_k_w128, cache_v_w128, cache_k_w512, cache_v_w512, cache_k_w2048,
              cache_v_w2048, state_conv, p_prompt, p_sample, g_mix, w_in, w_ao, conv_w, conv_b, conv_ln_g,
              conv_ln_b, w_cp, w_o, g_ffn, w_ff_gate, w_ff_up, w_ff_down, w_router, w_e_gate, w_e_up, w_e_down,
              g_ple, w_pg, w_ple, g_final):
    weights = (g_mix, w_in, w_ao, conv_w, conv_b, conv_ln_g, conv_ln_b, w_cp, w_o, g_ffn, w_ff_gate, w_ff_up,
               w_ff_down, w_router, w_e_gate, w_e_up, w_e_down, g_ple, w_pg, w_ple, g_final)
    pos_p = jnp.arange(x_prompt.shape[1], dtype=jnp.int32)
    pos_s = PAST_LEN + jnp.arange(x_sample.shape[1], dtype=jnp.int32)
    y_prompt, kp, vp, conv_p = run_trunk(x_prompt, p_prompt, pos_p, None, None, None, *weights)
    y_sample, ksm, vsm, conv_s = run_trunk(
        x_sample, p_sample, pos_s,
        [cache_k_w128, cache_k_w512, cache_k_w2048],
        [cache_v_w128, cache_v_w512, cache_v_w2048],
        state_conv, *weights)
    k128_p, k512_p, k2048_p = kp
    v128_p, v512_p, v2048_p = vp
    k128_s, k512_s, k2048_s = ksm
    v128_s, v512_s, v2048_s = vsm
    return (y_prompt, y_sample, k128_p, v128_p, k512_p, v512_p, k2048_p, v2048_p, conv_p,
            k128_s, v128_s, k512_s, v512_s, k2048_s, v2048_s, conv_s)
```

```python
import functools

import jax
import jax.numpy as jnp
from jax import lax
from jax.experimental import pallas as pl
from jax.experimental.pallas import tpu as pltpu

BF16 = jnp.bfloat16
F32 = jnp.float32

HEAD_DIM = 128
HEADS = 8
GROUPS = ((128, 1), (512, 4), (2048, 16))
KV = HEADS * HEAD_DIM
ATTN_WIDTH = len(GROUPS) * KV
BAND = 128
CONV_WIDTH = 31
N_EXPERTS = 8
TOP_K = 2
EPS = 1e-6
ROPE_THETA = 10000.0
PAST_LEN = 16384

LANES = 128
SAMPLE_ROWS = 16
VMEM_CAP = 60 * 1024 * 1024
MOE_BLOCK = 256


def _vmem_limit(nbytes):
    return int(min(VMEM_CAP, nbytes + (8 << 20)))


def _nbytes(shape, dtype):
    n = 1
    for s in shape:
        if s is not None:
            n *= s
    return n * jnp.dtype(dtype).itemsize


def _fused_matmul(acts, weights, extras, epilogue, out_dtypes, *, n_cols, tm, tn, name):
    m_rows = acts[0][0].shape[-2]
    ni, nj = m_rows // tm, n_cols // tn
    assert ni * tm == m_rows and nj * tn == n_cols
    na, nw, ne, no = len(acts), len(weights), len(extras), len(out_dtypes)

    in_specs, args, scratch, vmem = [], [], [], 0
    for arr, lead in acts:
        k = arr.shape[-1]
        bs = (None,) * len(lead) + (tm, k)
        in_specs.append(pl.BlockSpec(bs, lambda j, i, lead=lead: (*lead, i, 0)))
        args.append(arr)
        vmem += 2 * _nbytes(bs, arr.dtype)
    for arr, lead, col0, _ in weights:
        k = arr.shape[-2]
        bs = (None,) * len(lead) + (k, tn)
        in_specs.append(pl.BlockSpec(bs, lambda j, i, lead=lead, col0=col0: (*lead, 0, col0 + j)))
        args.append(arr)
        scratch.append(pltpu.VMEM((k, tn), BF16))
        vmem += 2 * _nbytes(bs, arr.dtype) + _nbytes((k, tn), BF16) + 2 * tm * tn * 4
    for arr, bs, imap in extras:
        in_specs.append(pl.BlockSpec(bs, imap))
        args.append(arr)
        vmem += 2 * _nbytes(bs, arr.dtype)
    out_specs = [pl.BlockSpec((tm, tn), lambda j, i: (i, j)) for _ in out_dtypes]
    out_shape = [jax.ShapeDtypeStruct((m_rows, n_cols), dt) for dt in out_dtypes]
    vmem += sum(2 * tm * tn * jnp.dtype(dt).itemsize for dt in out_dtypes)
    act_idx = [w[3] for w in weights]

    def body(*refs):
        act_refs = refs[:na]
        w_refs = refs[na:na + nw]
        ex_refs = refs[na + nw:na + nw + ne]
        out_refs = refs[na + nw + ne:na + nw + ne + no]
        wb_refs = refs[na + nw + ne + no:]

        @pl.when(pl.program_id(1) == 0)
        def _():
            for w_ref, wb in zip(w_refs, wb_refs):
                wb[...] = w_ref[...].astype(BF16)

        accs = []
        for wi in range(nw):
            a = act_refs[act_idx[wi]][...]
            if a.dtype != BF16:
                a = a.astype(BF16)
            accs.append(jnp.dot(a, wb_refs[wi][...], preferred_element_type=F32))
        epilogue(accs, ex_refs, out_refs)

    outs = pl.pallas_call(
        body,
        grid=(nj, ni),
        in_specs=in_specs,
        out_specs=out_specs,
        out_shape=out_shape,
        scratch_shapes=scratch,
        compiler_params=pltpu.CompilerParams(
            dimension_semantics=("arbitrary", "arbitrary"),
            vmem_limit_bytes=_vmem_limit(vmem)),
        name=name,
    )(*args)
    return outs


def _tile(arr, tm, tn):
    return (arr, (tm, tn), lambda j, i: (i, j))


def _epi_store(accs, ex, outs):
    outs[0][...] = accs[0].astype(outs[0].dtype)


def _epi_rope(accs, ex, outs):
    cos, sin = ex[0][...], ex[1][...]
    acc = accs[0]
    for h in range(acc.shape[1] // HEAD_DIM):
        t = acc[:, h * HEAD_DIM:(h + 1) * HEAD_DIM]
        r = t * cos + pltpu.roll(t, HEAD_DIM // 2, 1) * sin
        outs[0][:, h * HEAD_DIM:(h + 1) * HEAD_DIM] = r.astype(outs[0].dtype)


def _epi_sigmoid(accs, ex, outs):
    outs[0][...] = jax.nn.sigmoid(accs[0]).astype(outs[0].dtype)


def _epi_gated_sum(accs, ex, outs):
    outs[0][...] = (ex[0][...] * accs[0] + ex[1][...] * accs[1]).astype(outs[0].dtype)


def _epi_residual(accs, ex, outs):
    outs[0][...] = (ex[0][...] + accs[0]).astype(outs[0].dtype)


def _epi_swiglu(accs, ex, outs):
    g = accs[0]
    outs[0][...] = (g * jax.nn.sigmoid(g) * accs[1]).astype(outs[0].dtype)


def _epi_ple(accs, ex, outs):
    outs[0][...] = (ex[0][...] + jax.nn.sigmoid(accs[0]) * accs[1]).astype(outs[0].dtype)


def _rmsnorm_body(x_ref, g_ref, o_ref):
    x = x_ref[...]
    y = x * lax.rsqrt(jnp.mean(x * x, axis=-1, keepdims=True) + EPS) * g_ref[...]
    o_ref[...] = y.astype(o_ref.dtype)


def _rmsnorm(x, g, lead, out_dtype):
    m_rows, d = x.shape
    tm = min(m_rows, 512)
    return pl.pallas_call(
        _rmsnorm_body,
        grid=(m_rows // tm,),
        in_specs=[pl.BlockSpec((tm, d), lambda i: (i, 0)),
                  pl.BlockSpec((None,) * len(lead) + (1, d), lambda i: (*lead, 0, 0))],
        out_specs=pl.BlockSpec((tm, d), lambda i: (i, 0)),
        out_shape=jax.ShapeDtypeStruct((m_rows, d), out_dtype),
        name="rmsnorm",
    )(x, g)


def _attn_prompt_body(q_ref, k_ref, v_ref, o_ref, lse_ref, *, n_heads, seq):
    scale = HEAD_DIM ** -0.5
    nb = seq // BAND
    qi = lax.broadcasted_iota(jnp.int32, (BAND, 2 * BAND), 0)
    kj = lax.broadcasted_iota(jnp.int32, (BAND, 2 * BAND), 1)
    band_mask = (kj >= qi) & (kj <= qi + BAND)
    first_mask = (lax.broadcasted_iota(jnp.int32, (BAND, BAND), 1)
                  <= lax.broadcasted_iota(jnp.int32, (BAND, BAND), 0))
    for h in range(n_heads):
        cols = slice(h * HEAD_DIM, (h + 1) * HEAD_DIM)
        for b in range(nb):
            qb = q_ref[b * BAND:(b + 1) * BAND, cols]
            lo = max(b - 1, 0) * BAND
            kw = k_ref[lo:(b + 1) * BAND, cols].astype(BF16)
            vw = v_ref[lo:(b + 1) * BAND, cols].astype(BF16)
            s = lax.dot_general(qb, kw, (((1,), (1,)), ((), ())), preferred_element_type=F32) * scale
            s = jnp.where(first_mask if b == 0 else band_mask, s, -jnp.inf)
            m = jnp.max(s, axis=-1, keepdims=True)
            p = jnp.exp(s - m)
            l = jnp.sum(p, axis=-1, keepdims=True)
            o = jnp.dot(p.astype(BF16), vw, preferred_element_type=F32) / l
            o_ref[b * BAND:(b + 1) * BAND, cols] = o
            lse_ref[b * BAND:(b + 1) * BAND, cols] = jnp.broadcast_to(m + jnp.log(l), (BAND, HEAD_DIM))


def _attn_prompt(q, k, v, n_batch, seq, g):
    _, dil = GROUPS[g]
    length = seq // dil
    hb = 2 if dil == 1 else HEADS
    cb = hb * HEAD_DIM
    shp = (n_batch, length, dil * ATTN_WIDTH)
    q3, k3, v3 = q.reshape(shp), k.reshape(shp), v.reshape(shp)

    def in_map(n, r, c):
        return (n, 0, (r * ATTN_WIDTH + g * KV) // cb + c)

    def out_map(n, r, c):
        return (n, 0, (r * KV) // cb + c)

    vmem = 2 * length * cb * (2 + 4 + 4 + 4 + 4)
    o, lse = pl.pallas_call(
        functools.partial(_attn_prompt_body, n_heads=hb, seq=length),
        grid=(n_batch, dil, HEADS // hb),
        in_specs=[pl.BlockSpec((None, length, cb), in_map)] * 3,
        out_specs=[pl.BlockSpec((None, length, cb), out_map)] * 2,
        out_shape=[jax.ShapeDtypeStruct((n_batch, length, dil * KV), F32)] * 2,
        compiler_params=pltpu.CompilerParams(vmem_limit_bytes=_vmem_limit(vmem)),
        name=f"attn_prompt_d{dil}",
    )(q3, k3, v3)
    return o.reshape(n_batch * seq, KV), lse.reshape(n_batch * seq, KV)


def _longnet_body(o0, o1, o2, l0, l1, l2, out_ref):
    a, b, c = l0[...], l1[...], l2[...]
    m = jnp.maximum(jnp.maximum(a, b), c)
    ea, eb, ec = jnp.exp(a - m), jnp.exp(b - m), jnp.exp(c - m)
    den = ea + eb + ec
    out_ref[...] = ((ea * o0[...] + eb * o1[...] + ec * o2[...]) / den).astype(out_ref.dtype)


def _longnet_mix(os_, lses):
    m_rows = os_[0].shape[0]
    tm = 512
    spec = pl.BlockSpec((tm, KV), lambda i: (i, 0))
    return pl.pallas_call(
        _longnet_body,
        grid=(m_rows // tm,),
        in_specs=[spec] * 6,
        out_specs=spec,
        out_shape=jax.ShapeDtypeStruct((m_rows, KV), BF16),
        compiler_params=pltpu.CompilerParams(vmem_limit_bytes=_vmem_limit(2 * 7 * tm * KV * 4)),
        name="longnet_mix",
    )(*os_, *lses)


def _attn_step_body(q_ref, kn_ref, vn_ref, k0, v0, k1, v1, k2, v2, o_ref):
    scale = HEAD_DIM ** -0.5
    caches = ((k0, v0), (k1, v1), (k2, v2))
    for h in range(HEADS):
        outs, lses = [], []
        for g in range(len(GROUPS)):
            cols = slice(g * KV + h * HEAD_DIM, g * KV + (h + 1) * HEAD_DIM)
            hc = slice(h * HEAD_DIM, (h + 1) * HEAD_DIM)
            qh = q_ref[:, cols].astype(F32)
            kc, vc = caches[g][0][:, hc], caches[g][1][:, hc]
            kn, vn = kn_ref[:, cols], vn_ref[:, cols]
            s_c = jnp.sum(kc * qh, axis=-1, keepdims=True) * scale
            s_n = jnp.sum(kn * qh, axis=-1, keepdims=True) * scale
            m = jnp.maximum(jnp.max(s_c, axis=0, keepdims=True), s_n)
            p_c, p_n = jnp.exp(s_c - m), jnp.exp(s_n - m)
            l = jnp.sum(p_c, axis=0, keepdims=True) + p_n
            outs.append((jnp.sum(p_c * vc, axis=0, keepdims=True) + p_n * vn) / l)
            lses.append(m + jnp.log(l))
        m = jnp.maximum(jnp.maximum(lses[0], lses[1]), lses[2])
        es = [jnp.exp(x - m) for x in lses]
        den = es[0] + es[1] + es[2]
        o = (es[0] * outs[0] + es[1] * outs[1] + es[2] * outs[2]) / den
        o_ref[:, h * HEAD_DIM:(h + 1) * HEAD_DIM] = o.astype(o_ref.dtype)


def _attn_step(q, kn, vn, caches, layer):
    nb = q.shape[0]
    new_spec = pl.BlockSpec((None, 1, ATTN_WIDTH), lambda n: (n, 0, 0))
    in_specs, args = [new_spec] * 3, [q, kn, vn]
    for (kc, vc), (_, dil) in zip(caches, GROUPS):
        depth, _, wb = kc.shape[:3]
        view = (depth, nb, wb // dil, dil * KV)
        spec = pl.BlockSpec((None, None, BAND, KV), lambda n: (layer, n, 0, 0))
        in_specs += [spec, spec]
        args += [kc.reshape(view), vc.reshape(view)]
    return pl.pallas_call(
        _attn_step_body,
        grid=(nb,),
        in_specs=in_specs,
        out_specs=pl.BlockSpec((None, 1, KV), lambda n: (n, 0, 0)),
        out_shape=jax.ShapeDtypeStruct((nb, 1, KV), BF16),
        name="attn_step",
    )(*args)


def _cache_shift_body(a_ref, b_ref, new_ref, o_ref, *, rb):
    last = pl.program_id(2) == pl.num_programs(2) - 1
    o_ref[0:rb - 1] = a_ref[1:rb]
    o_ref[rb - 1:rb] = jnp.where(last, new_ref[...], b_ref[...])


def _cache_shift(cache, new):
    depth, nb, wb = cache.shape[:3]
    rb = min(wb, 512)
    tail = (None, None, 1, HEADS, HEAD_DIM)
    return pl.pallas_call(
        functools.partial(_cache_shift_body, rb=rb),
        grid=(depth, nb, wb // rb),
        in_specs=[pl.BlockSpec((None, None, rb, HEADS, HEAD_DIM), lambda l, n, r: (l, n, r, 0, 0)),
                  pl.BlockSpec(tail, lambda l, n, r: (l, n, jnp.minimum((r + 1) * rb, wb - 1), 0, 0)),
                  pl.BlockSpec(tail, lambda l, n, r: (l, n, 0, 0, 0))],
        out_specs=pl.BlockSpec((None, None, rb, HEADS, HEAD_DIM), lambda l, n, r: (l, n, r, 0, 0)),
        out_shape=jax.ShapeDtypeStruct(cache.shape, cache.dtype),
        name="cache_shift",
    )(cache, cache, new)


CONV_TT = 256
CONV_HALO = 32
CONV_RC = 32


def _glu(cu):
    dc = cu.shape[-1] // 2
    return cu[..., :dc] * jax.nn.sigmoid(cu[..., dc:])


def _ln_swish(y, g, b):
    mu = jnp.mean(y, axis=-1, keepdims=True)
    yc = y - mu
    var = jnp.mean(yc * yc, axis=-1, keepdims=True)
    z = yc * lax.rsqrt(var + EPS) * g + b
    return z * jax.nn.sigmoid(z)


def _conv_prompt_body(cur_ref, prev_ref, w_ref, b_ref, g_ref, beta_ref, c_ref, tail_ref, ext, ybuf):
    t = pl.program_id(1)
    prev_u = _glu(prev_ref[...])
    ext[0:CONV_HALO, :] = jnp.where(t == 0, jnp.zeros_like(prev_u), prev_u)
    ext[CONV_HALO:, :] = _glu(cur_ref[...])
    off = CONV_HALO - (CONV_WIDTH - 1)
    for r0 in range(0, CONV_TT, CONV_RC):
        acc = jnp.broadcast_to(b_ref[...], (CONV_RC, b_ref.shape[-1]))
        for j in range(CONV_WIDTH):
            acc = acc + w_ref[j:j + 1, :] * ext[r0 + off + j:r0 + off + j + CONV_RC, :]
        ybuf[r0:r0 + CONV_RC, :] = acc
    c_ref[...] = _ln_swish(ybuf[...], g_ref[...], beta_ref[...]).astype(c_ref.dtype)

    @pl.when(t == pl.num_programs(1) - 1)
    def _():
        tail_ref[...] = ext[CONV_TT + off:, :]


def _conv_prompt(cu, conv_w, conv_b, ln_g, ln_b, layer, n_batch, seq):
    dc = cu.shape[-1] // 2
    cu3 = cu.reshape(n_batch, seq, 2 * dc)
    vec = pl.BlockSpec((None, 1, dc), lambda n, t: (layer, 0, 0))
    ratio = CONV_TT // CONV_HALO
    c, tail = pl.pallas_call(
        _conv_prompt_body,
        grid=(n_batch, seq // CONV_TT),
        in_specs=[pl.BlockSpec((None, CONV_TT, 2 * dc), lambda n, t: (n, t, 0)),
                  pl.BlockSpec((None, CONV_HALO, 2 * dc), lambda n, t: (n, jnp.maximum(t * ratio - 1, 0), 0)),
                  pl.BlockSpec((None, CONV_WIDTH, dc), lambda n, t: (layer, 0, 0)),
                  vec, vec, vec],
        out_specs=[pl.BlockSpec((None, CONV_TT, dc), lambda n, t: (n, t, 0)),
                   pl.BlockSpec((None, CONV_WIDTH - 1, dc), lambda n, t: (n, 0, 0))],
        out_shape=[jax.ShapeDtypeStruct((n_batch, seq, dc), BF16),
                   jax.ShapeDtypeStruct((n_batch, CONV_WIDTH - 1, dc), F32)],
        scratch_shapes=[pltpu.VMEM((CONV_HALO + CONV_TT, dc), F32), pltpu.VMEM((CONV_TT, dc), F32)],
        name="conv_prompt",
    )(cu3, cu3, conv_w, conv_b, ln_g, ln_b)
    return c.reshape(n_batch * seq, dc), tail


def _conv_step_body(cu_ref, hist_ref, w_ref, b_ref, g_ref, beta_ref, c_ref, new_ref):
    nh = CONV_WIDTH - 1
    u = _glu(cu_ref[...])
    hist = hist_ref[...]
    y = b_ref[...] + jnp.sum(w_ref[0:nh, :] * hist, axis=0, keepdims=True) + w_ref[nh:nh + 1, :] * u
    c_ref[...] = _ln_swish(y, g_ref[...], beta_ref[...]).astype(c_ref.dtype)
    new_ref[0:nh - 1, :] = hist_ref[1:nh, :]
    new_ref[nh - 1:nh, :] = u


def _conv_step(cu, state_conv, conv_w, conv_b, ln_g, ln_b, layer):
    nb, _, dc2 = cu.shape
    dc = dc2 // 2
    nh = CONV_WIDTH - 1
    vec = pl.BlockSpec((None, 1, dc), lambda n: (layer, 0, 0))
    return pl.pallas_call(
        _conv_step_body,
        grid=(nb,),
        in_specs=[pl.BlockSpec((None, 1, dc2), lambda n: (n, 0, 0)),
                  pl.BlockSpec((None, None, nh, dc), lambda n: (layer, n, 0, 0)),
                  pl.BlockSpec((None, CONV_WIDTH, dc), lambda n: (layer, 0, 0)),
                  vec, vec, vec],
        out_specs=[pl.BlockSpec((None, 1, dc), lambda n: (n, 0, 0)),
                   pl.BlockSpec((None, nh, dc), lambda n: (n, 0, 0))],
        out_shape=[jax.ShapeDtypeStruct((nb, 1, dc), BF16),
                   jax.ShapeDtypeStruct((nb, nh, dc), F32)],
        name="conv_step",
    )(cu, state_conv, conv_w, conv_b, ln_g, ln_b)


def _router_body(h_ref, w_ref, idx_ref, gate_ref):
    logits = jnp.dot(h_ref[...], w_ref[...].astype(BF16), preferred_element_type=F32)
    lane = lax.broadcasted_iota(jnp.int32, logits.shape, 1)
    logits = jnp.where(lane < N_EXPERTS, logits, -jnp.inf)
    m1 = jnp.max(logits, axis=-1, keepdims=True)
    i1 = jnp.min(jnp.where(logits == m1, lane, LANES), axis=-1, keepdims=True)
    rest = jnp.where(lane == i1, -jnp.inf, logits)
    m2 = jnp.max(rest, axis=-1, keepdims=True)
    i2 = jnp.min(jnp.where(rest == m2, lane, LANES), axis=-1, keepdims=True)
    e2 = jnp.exp(m2 - m1)
    den = 1.0 + e2
    idx_ref[...] = jnp.where(lane == 0, i1, jnp.where(lane == 1, i2, 0))
    gate_ref[...] = jnp.where(lane == 0, 1.0 / den, jnp.where(lane == 1, e2 / den, 0.0))


def _router(h, w_router_padded, layer):
    m_rows, d = h.shape
    tm = min(m_rows, 1024)
    out_spec = pl.BlockSpec((tm, LANES), lambda i: (i, 0))
    return pl.pallas_call(
        _router_body,
        grid=(m_rows // tm,),
        in_specs=[pl.BlockSpec((tm, d), lambda i: (i, 0)),
                  pl.BlockSpec((None, d, LANES), lambda i: (layer, 0, 0))],
        out_specs=[out_spec, out_spec],
        out_shape=[jax.ShapeDtypeStruct((m_rows, LANES), jnp.int32),
                   jax.ShapeDtypeStruct((m_rows, LANES), F32)],
        name="router",
    )(h, w_router_padded)


def _grouped_matmul(x, weights, row_scale, block_e, n_used, epilogue, out_dtype, *, layer, bm, tn, name):
    rows, k = x.shape
    n_cols = weights[0].shape[-1]
    nb, nj, nw = rows // bm, n_cols // tn, len(weights)
    in_specs = [pl.BlockSpec((bm, k), lambda j, b, be, nu: (b, 0))]
    in_specs += [pl.BlockSpec((None, None, k, tn), lambda j, b, be, nu: (layer, be[b], 0, j))] * nw
    args = [x, *weights]
    if row_scale is not None:
        in_specs.append(pl.BlockSpec((bm, 1), lambda j, b, be, nu: (b, 0)))
        args.append(row_scale)
    vmem = 2 * bm * k * 2 + nw * (2 * k * tn * 4 + k * tn * 2 + 2 * bm * tn * 4) + 2 * bm * tn * 4

    def body(be_ref, nu_ref, x_ref, *refs):
        w_refs = refs[:nw]
        rest = refs[nw:]
        s_ref = rest[0] if row_scale is not None else None
        o_ref = rest[-1 - nw]
        wb_refs = rest[-nw:]
        b = pl.program_id(1)
        used = b < nu_ref[0]
        changed = (b == 0) | (be_ref[b] != be_ref[jnp.maximum(b - 1, 0)])

        @pl.when(used & changed)
        def _():
            for w_ref, wb in zip(w_refs, wb_refs):
                wb[...] = w_ref[...].astype(BF16)

        @pl.when(used)
        def _():
            xb = x_ref[...]
            accs = [jnp.dot(xb, wb[...], preferred_element_type=F32) for wb in wb_refs]
            y = epilogue(accs)
            if s_ref is not None:
                y = y * s_ref[...]
            o_ref[...] = y.astype(o_ref.dtype)

        @pl.when(jnp.logical_not(used))
        def _():
            o_ref[...] = jnp.zeros_like(o_ref)

    return pl.pallas_call(
        body,
        grid_spec=pltpu.PrefetchScalarGridSpec(
            num_scalar_prefetch=2,
            grid=(nj, nb),
            in_specs=in_specs,
            out_specs=pl.BlockSpec((bm, tn), lambda j, b, be, nu: (b, j)),
            scratch_shapes=[pltpu.VMEM((k, tn), BF16)] * nw),
        out_shape=jax.ShapeDtypeStruct((rows, n_cols), out_dtype),
        compiler_params=pltpu.CompilerParams(
            dimension_semantics=("arbitrary", "arbitrary"),
            vmem_limit_bytes=_vmem_limit(vmem)),
        name=name,
    )(block_e, n_used, *args)


def _swiglu_act(accs):
    return accs[0] * jax.nn.sigmoid(accs[0]) * accs[1]


def _first(accs):
    return accs[0]


def _moe_dispatch(top_i, gates, bm):
    n_tok = top_i.shape[0]
    n_asg = n_tok * TOP_K
    e_flat = top_i.reshape(-1)
    g_flat = gates.reshape(-1)
    tok_flat = jnp.arange(n_asg, dtype=jnp.int32) // TOP_K
    onehot = (e_flat[:, None] == jnp.arange(N_EXPERTS, dtype=jnp.int32)[None, :]).astype(jnp.int32)
    counts = jnp.sum(onehot, axis=0)
    rank = jnp.sum((jnp.cumsum(onehot, axis=0) - onehot) * onehot, axis=1)
    padded = ((counts + bm - 1) // bm) * bm
    pad_end = jnp.cumsum(padded)
    pad_start = pad_end - padded
    dest = pad_start[e_flat] + rank
    nblk = -(-n_asg // bm) + N_EXPERTS
    row_tok = jnp.zeros((nblk * bm,), jnp.int32).at[dest].set(tok_flat)
    row_gate = jnp.zeros((nblk * bm,), F32).at[dest].set(g_flat)
    block_e = jnp.minimum(jnp.searchsorted(pad_end, jnp.arange(nblk, dtype=jnp.int32) * bm, side='right'),
                          N_EXPERTS - 1).astype(jnp.int32)
    n_used = (pad_end[-1:] // bm).astype(jnp.int32)
    return dest.reshape(n_tok, TOP_K), row_tok, row_gate, block_e, n_used


def _moe(h_all, top_i, gates, w_e_gate, w_e_up, w_e_down, j):
    dest, row_tok, row_gate, block_e, n_used = _moe_dispatch(top_i, gates, MOE_BLOCK)
    xbuf = jnp.take(h_all, row_tok, axis=0)
    act = _grouped_matmul(xbuf, [w_e_gate, w_e_up], None, block_e, n_used, _swiglu_act, BF16,
                          layer=j, bm=MOE_BLOCK, tn=512, name="moe_up")
    ybuf = _grouped_matmul(act, [w_e_down], row_gate[:, None], block_e, n_used, _first, F32,
                           layer=j, bm=MOE_BLOCK, tn=512, name="moe_down")
    return jnp.take(ybuf, dest[:, 0], axis=0) + jnp.take(ybuf, dest[:, 1], axis=0)


def _rope_tables(pos):
    half = HEAD_DIM // 2
    inv = ROPE_THETA ** (-jnp.arange(half, dtype=F32) / half)
    ang = pos.astype(F32)[:, None] * inv[None, :]
    cos, sin = jnp.cos(ang), jnp.sin(ang)
    return jnp.concatenate([cos, cos], axis=-1), jnp.concatenate([-sin, sin], axis=-1)


def _in_projection(h, w_in, layer, cos, sin, n_table_blocks, tm, d_conv, d_model):
    tn = 512
    act = [(h, ())]
    rope_ex = [(cos, (tm, HEAD_DIM), lambda j, i: (i % n_table_blocks, 0)),
               (sin, (tm, HEAD_DIM), lambda j, i: (i % n_table_blocks, 0))]

    def call(col0, n_cols, extras, epi, dt, name):
        return _fused_matmul(act, [(w_in, (layer,), col0 // tn, 0)], extras, epi, [dt],
                             n_cols=n_cols, tm=tm, tn=tn, name=name)[0]

    q = call(0, ATTN_WIDTH, rope_ex, _epi_rope, BF16, "in_q")
    k = call(ATTN_WIDTH, ATTN_WIDTH, rope_ex, _epi_rope, F32, "in_k")
    v = call(2 * ATTN_WIDTH, ATTN_WIDTH, [], _epi_store, F32, "in_v")
    cu = call(3 * ATTN_WIDTH, 2 * d_conv, [], _epi_store, F32, "in_cu")
    gates = call(3 * ATTN_WIDTH + 2 * d_conv, 2 * d_model, [], _epi_sigmoid, F32, "in_gates")
    return q, k, v, cu, gates


def _mix_and_project(x, o, c, gates, w_ao, w_cp, w_o, layer, tm, d_model):
    tn = 512
    nj = d_model // tn
    sga = (gates, (tm, tn), lambda j, i: (i, j))
    sgc = (gates, (tm, tn), lambda j, i: (i, nj + j))
    mixed = _fused_matmul([(o, ()), (c, ())],
                          [(w_ao, (layer,), 0, 0), (w_cp, (layer,), 0, 1)],
                          [sga, sgc], _epi_gated_sum, [BF16],
                          n_cols=d_model, tm=tm, tn=tn, name="attn_conv_out")[0]
    return _fused_matmul([(mixed, ())], [(w_o, (layer,), 0, 0)], [_tile(x, tm, tn)], _epi_residual, [F32],
                         n_cols=d_model, tm=tm, tn=tn, name="merge_out")[0]


def _dense_ffn(x, h2, w_gate, w_up, w_down, j, tm):
    d_ff = w_gate.shape[-1]
    d_model = x.shape[-1]
    act = _fused_matmul([(h2, ())], [(w_gate, (j,), 0, 0), (w_up, (j,), 0, 0)], [], _epi_swiglu, [BF16],
                        n_cols=d_ff, tm=tm, tn=512, name="ffn_up")[0]
    tn = 256
    return _fused_matmul([(act, ())], [(w_down, (j,), 0, 0)], [_tile(x, tm, tn)], _epi_residual, [F32],
                         n_cols=d_model, tm=tm, tn=tn, name="ffn_down")[0]


def _ple(x, h3, p, w_pg, w_ple, layer, tm):
    tn = 512
    d_model = x.shape[-1]
    return _fused_matmul([(h3, ()), (p, (layer,))],
                         [(w_pg, (layer,), 0, 0), (w_ple, (layer,), 0, 1)],
                         [_tile(x, tm, tn)], _epi_ple, [F32],
                         n_cols=d_model, tm=tm, tn=tn, name="ple")[0]


def kernel(x_prompt, x_sample, cache_k_w128, cache_v_w128, cache_k_w512, cache_v_w512, cache_k_w2048, cache_v_w2048, state_conv, p_prompt, p_sample, g_mix, w_in, w_ao, conv_w, conv_b, conv_ln_g, conv_ln_b, w_cp, w_o, g_ffn, w_ff_gate, w_ff_up, w_ff_down, w_router, w_e_gate, w_e_up, w_e_down, g_ple, w_pg, w_ple, g_final):
    n_batch, seq, d_model = x_prompt.shape
    n_dec = x_sample.shape[0]
    assert x_sample.shape[1] == 1
    depth = w_in.shape[0]
    d_conv = conv_w.shape[-1]
    ple_dim = p_prompt.shape[-1]
    mp = n_batch * seq
    tm_p, tm_s = 1024, SAMPLE_ROWS
    pad_s = SAMPLE_ROWS - n_dec

    caches = [(cache_k_w128, cache_v_w128), (cache_k_w512, cache_v_w512), (cache_k_w2048, cache_v_w2048)]

    cos_p, sin_p = _rope_tables(jnp.arange(seq, dtype=jnp.int32))
    cos_s, sin_s = _rope_tables(jnp.full((SAMPLE_ROWS,), PAST_LEN, dtype=jnp.int32))
    row = lambda a: a.reshape(a.shape[:-1] + (1, a.shape[-1]))
    g_mix_r, g_ffn_r, g_ple_r = row(g_mix), row(g_ffn), row(g_ple)
    conv_b_r, ln_g_r, ln_b_r = row(conv_b), row(conv_ln_g), row(conv_ln_b)
    w_router_p = jnp.pad(w_router, ((0, 0), (0, 0), (0, LANES - N_EXPERTS)))

    xp = x_prompt.reshape(mp, d_model)
    xs = jnp.pad(x_sample.reshape(n_dec, d_model), ((0, pad_s), (0, 0)))
    pp = p_prompt.reshape(depth, mp, ple_dim)
    ps = jnp.pad(p_sample.reshape(depth, n_dec, ple_dim), ((0, 0), (0, pad_s), (0, 0)))

    kp_out = [[] for _ in GROUPS]
    vp_out = [[] for _ in GROUPS]
    ks_new = [[] for _ in GROUPS]
    vs_new = [[] for _ in GROUPS]
    conv_p_out, conv_s_out = [], []

    for i in range(depth):
        hp = _rmsnorm(xp, g_mix_r, (i,), BF16)
        q, k, v, cu, gates = _in_projection(hp, w_in, i, cos_p, sin_p, seq // tm_p, tm_p, d_conv, d_model)
        os_, lses = [], []
        for g, (window, _) in enumerate(GROUPS):
            o_g, lse_g = _attn_prompt(q, k, v, n_batch, seq, g)
            os_.append(o_g)
            lses.append(lse_g)
            wb = min(window, seq)
            k4 = k.reshape(n_batch, seq, len(GROUPS), HEADS, HEAD_DIM)
            v4 = v.reshape(n_batch, seq, len(GROUPS), HEADS, HEAD_DIM)
            kp_out[g].append(k4[:, seq - wb:, g])
            vp_out[g].append(v4[:, seq - wb:, g])
        o = _longnet_mix(os_, lses)
        c, tail = _conv_prompt(cu, conv_w, conv_b_r, ln_g_r, ln_b_r, i, n_batch, seq)
        conv_p_out.append(tail)
        xp = _mix_and_project(xp, o, c, gates, w_ao, w_cp, w_o, i, tm_p, d_model)

        hs = _rmsnorm(xs, g_mix_r, (i,), BF16)
        q, k, v, cu, gates = _in_projection(hs, w_in, i, cos_s, sin_s, 1, tm_s, d_conv, d_model)
        three = lambda a: a[:n_dec].reshape(n_dec, 1, a.shape[-1])
        o = _attn_step(three(q).astype(F32), three(k), three(v), caches, i)
        for g in range(len(GROUPS)):
            ks_new[g].append(k[:n_dec, g * KV:(g + 1) * KV].reshape(n_dec, 1, HEADS, HEAD_DIM))
            vs_new[g].append(v[:n_dec, g * KV:(g + 1) * KV].reshape(n_dec, 1, HEADS, HEAD_DIM))
        c, new_state = _conv_step(three(cu), state_conv, conv_w, conv_b_r, ln_g_r, ln_b_r, i)
        conv_s_out.append(new_state)
        pad_rows = lambda a: jnp.pad(a.reshape(n_dec, a.shape[-1]), ((0, pad_s), (0, 0)))
        xs = _mix_and_project(xs, pad_rows(o), pad_rows(c), gates, w_ao, w_cp, w_o, i, tm_s, d_model)

        h2p = _rmsnorm(xp, g_ffn_r, (i,), BF16)
        h2s = _rmsnorm(xs, g_ffn_r, (i,), BF16)
        j = i // 2
        if i % 2 == 0:
            xp = _dense_ffn(xp, h2p, w_ff_gate, w_ff_up, w_ff_down, j, tm_p)
            xs = _dense_ffn(xs, h2s, w_ff_gate, w_ff_up, w_ff_down, j, tm_s)
        else:
            ip, gp = _router(h2p, w_router_p, j)
            is_, gs = _router(h2s, w_router_p, j)
            top_i = jnp.concatenate([ip[:, :TOP_K], is_[:n_dec, :TOP_K]], axis=0)
            gate = jnp.concatenate([gp[:, :TOP_K], gs[:n_dec, :TOP_K]], axis=0)
            h_all = jnp.concatenate([h2p, h2s[:n_dec]], axis=0)
            y_all = _moe(h_all, top_i, gate, w_e_gate, w_e_up, w_e_down, j)
            xp = xp + y_all[:mp]
            xs = xs + jnp.pad(y_all[mp:], ((0, pad_s), (0, 0)))

        h3p = _rmsnorm(xp, g_ple_r, (i,), BF16)
        h3s = _rmsnorm(xs, g_ple_r, (i,), BF16)
        xp = _ple(xp, h3p, pp, w_pg, w_ple, i, tm_p)
        xs = _ple(xs, h3s, ps, w_pg, w_ple, i, tm_s)

    g_final_r = g_final.reshape(1, d_model)
    y_prompt = _rmsnorm(xp, g_final_r, (), F32).reshape(n_batch, seq, d_model)
    y_sample = _rmsnorm(xs, g_final_r, (), F32)[:n_dec].reshape(n_dec, 1, d_model)

    prompt_caches = []
    sample_caches = []
    for g in range(len(GROUPS)):
        prompt_caches += [jnp.stack(kp_out[g], axis=0), jnp.stack(vp_out[g], axis=0)]
        sample_caches += [_cache_shift(caches[g][0], jnp.stack(ks_new[g], axis=0)),
                          _cache_shift(caches[g][1], jnp.stack(vs_new[g], axis=0))]
    return (y_prompt, y_sample, *prompt_caches, jnp.stack(conv_p_out, axis=0),
            *sample_caches, jnp.stack(conv_s_out, axis=0))
```
